```python
import jax, jax.numpy as jnp
from jax import lax
import numpy as np

D_MODEL = 2048
BATCH = 4
SEQ = 4096
DEPTH = 4
DEC_BATCH = 16
DEC_SEQ = 32
PAST_LEN = 2048

CHUNK = 64
N_MIXERS = 2
N_HEADS = 16
HEAD_DIM = D_MODEL // N_HEADS
N_ATTN = (DEPTH + 1) // 2
N_HGRN = DEPTH // 2
Q_BLOCK = 128
SUB_CHUNK = 16
N_PROJ = 4
EPS = 1e-6
LB_FLOOR = 1e-30

kernel_name = 'stickbreak_hgrn2_streaming_step'


def rms_norm(x, w):
    xf = x.astype(jnp.float32)
    y = xf * lax.rsqrt(jnp.mean(xf * xf, axis=-1, keepdims=True) + EPS)
    return (y * w.astype(jnp.float32)).astype(x.dtype)


def stick_breaking_block(q, k, v, q_pos, k_pos):
    z = jnp.einsum('bqhd,bkhd->bhqk', q, k).astype(jnp.float32) * (HEAD_DIM ** -0.5)
    earlier = k_pos[None, :] < q_pos[:, None]
    log_one_minus = jnp.where(earlier, jax.nn.log_sigmoid(-z), 0.0)
    between = lax.cumsum(log_one_minus, axis=3, reverse=True) - log_one_minus
    w = jnp.where(earlier, jnp.exp(jnp.where(earlier, jax.nn.log_sigmoid(z) + between, 0.0)), 0.0)
    return jnp.einsum('bhqk,bkhd->bqhd', w.astype(v.dtype), v)


def stick_breaking(q, k, v, q_offset):
    tq = q.shape[1]
    outs = []
    for start in range(0, tq, Q_BLOCK):
        end = min(start + Q_BLOCK, tq)
        kend = q_offset + end
        outs.append(stick_breaking_block(q[:, start:end], k[:, :kend], v[:, :kend],
                                         jnp.arange(q_offset + start, q_offset + end), jnp.arange(kend)))
    return jnp.concatenate(outs, axis=1)


def hgrn2_chunk(S, inp):
    q, k, v, g = inp
    bn, h, c, dk = q.shape
    n = c // SUB_CHUNK
    b = jnp.cumsum(g, axis=2)
    o = jnp.einsum('bhtk,bhkv->bhtv', q * jnp.exp(b), S)
    qs = q.reshape(bn, h, n, SUB_CHUNK, dk)
    ks = k.reshape(bn, h, n, SUB_CHUNK, dk)
    bs = b.reshape(bn, h, n, SUB_CHUNK, dk)
    b_ref = jnp.concatenate([jnp.zeros_like(bs[:, :, :1, 0]), bs[:, :, :-1, -1]], axis=2)
    q_rel = qs * jnp.exp(jnp.minimum(bs - b_ref[:, :, :, None], 0.0))
    k_rel = ks[:, :, None] * jnp.exp(jnp.minimum(b_ref[:, :, :, None, None] - bs[:, :, None], 0.0))
    sub = jnp.arange(n)
    off_mask = (sub[None, :] < sub[:, None])[:, None, :, None]
    off = jnp.einsum('bhaik,bhacjk->bhaicj', q_rel, k_rel) * off_mask
    pos = jnp.arange(SUB_CHUNK)
    tri = pos[:, None] >= pos[None, :]
    decay = jnp.exp(jnp.minimum(bs[:, :, :, :, None] - bs[:, :, :, None, :], 0.0))
    diag = jnp.einsum('bhaik,bhajk,bhaijk->bhaij', qs, ks, decay) * tri
    scores = off + jnp.einsum('bhaij,ac->bhaicj', diag, jnp.eye(n, dtype=diag.dtype))
    o = o + jnp.einsum('bhts,bhsv->bhtv', scores.reshape(bn, h, c, c), v)
    b_last = b[:, :, -1]
    S_new = jnp.exp(b_last)[..., None] * S + jnp.einsum('bhsk,bhsv->bhkv', k * jnp.exp(jnp.minimum(b_last[:, :, None] - b, 0.0)), v)
    return S_new, o


def hgrn2(q, k, v, logf, S0):
    bn, t = q.shape[:2]
    nc = -(-t // CHUNK)
    pad = nc * CHUNK - t

    def prep(a):
        a = jnp.pad(a.astype(jnp.float32), ((0, 0), (0, pad), (0, 0), (0, 0)))
        return a.reshape(bn, nc, CHUNK, N_HEADS, HEAD_DIM).transpose(1, 0, 3, 2, 4)

    S, o = lax.scan(hgrn2_chunk, S0.astype(jnp.float32), (prep(q), prep(k), prep(v), prep(logf)))
    o = o.transpose(1, 0, 3, 2, 4).reshape(bn, nc * CHUNK, N_HEADS, HEAD_DIM)[:, :t]
    return o, S


def attn_mixer(h, w_in, q_w, k_w, past_k, past_v):
    bn, t = h.shape[:2]
    qa, ka, va, g = jnp.split(h @ w_in, N_PROJ, axis=-1)
    q = rms_norm(qa.reshape(bn, t, N_HEADS, HEAD_DIM), q_w)
    k = rms_norm(ka.reshape(bn, t, N_HEADS, HEAD_DIM), k_w)
    v = va.reshape(bn, t, N_HEADS, HEAD_DIM)
    if past_k is None:
        kk, vv, off = k, v, 0
    else:
        kk = jnp.concatenate([past_k.astype(k.dtype), k], axis=1)
        vv = jnp.concatenate([past_v.astype(v.dtype), v], axis=1)
        off = past_k.shape[1]
    o = stick_breaking(q, kk, vv, off).reshape(bn, t, D_MODEL)
    return o * jax.nn.silu(g), k, v


def hgrn_mixer(h, w_in, lb, out_w, S0):
    bn, t = h.shape[:2]
    qa, fa, ia, g = jnp.split(h @ w_in, N_PROJ, axis=-1)
    f32 = fa.astype(jnp.float32)
    logf = jnp.logaddexp(jnp.log(jnp.maximum(lb, LB_FLOOR)), jnp.log1p(-lb) + jax.nn.log_sigmoid(f32))
    logf = jnp.minimum(logf, 0.0)
    kk = (1.0 - lb) * jax.nn.sigmoid(-f32)
    heads = lambda a: a.reshape(bn, t, N_HEADS, HEAD_DIM)
    o, S = hgrn2(heads(jax.nn.silu(qa)), heads(kk), heads(ia), heads(logf), S0)
    o = rms_norm(o, out_w).reshape(bn, t, D_MODEL).astype(h.dtype)
    return o * jax.nn.silu(g), S


def setup_inputs(seed: int = 0) -> dict:
    key = jax.random.key(seed)
    ks = jax.random.split(key, 13)
    nrm = jax.random.normal
    scale = D_MODEL ** -0.5
    return {
        'x_prompt': nrm(ks[0], (BATCH, SEQ, D_MODEL), jnp.float32),
        'x_sample': nrm(ks[1], (DEC_BATCH, DEC_SEQ, D_MODEL), jnp.float32),
        'cache_k': nrm(ks[2], (N_ATTN, DEC_BATCH, PAST_LEN, N_HEADS, HEAD_DIM), jnp.float32),
        'cache_v': nrm(ks[3], (N_ATTN, DEC_BATCH, PAST_LEN, N_HEADS, HEAD_DIM), jnp.float32),
        'state_hgrn': 0.5 * nrm(ks[4], (N_HGRN, DEC_BATCH, N_HEADS, HEAD_DIM, HEAD_DIM), jnp.float32),
        'norm_w': 1.0 + 0.02 * nrm(ks[5], (DEPTH, D_MODEL), jnp.float32),
        'w_in': scale * nrm(ks[6], (DEPTH, D_MODEL, N_PROJ * D_MODEL), jnp.float32),
        'w_out': scale * nrm(ks[7], (DEPTH, D_MODEL, D_MODEL), jnp.float32),
        'q_norm_w': 1.0 + 0.02 * nrm(ks[8], (N_ATTN, HEAD_DIM), jnp.float32),
        'k_norm_w': 1.0 + 0.02 * nrm(ks[9], (N_ATTN, HEAD_DIM), jnp.float32),
        'hgrn_norm_w': 1.0 + 0.02 * nrm(ks[10], (N_HGRN, HEAD_DIM), jnp.float32),
        'hgrn_lb': 0.5 * nrm(ks[11], (N_HGRN, D_MODEL), jnp.float32),
    }


def reference(x_prompt, x_sample, cache_k, cache_v, state_hgrn, norm_w, w_in, w_out,
              q_norm_w, k_norm_w, hgrn_norm_w, hgrn_lb):
    lb_sm = jax.nn.softmax(hgrn_lb.astype(jnp.float32), axis=0)
    lbs = jnp.clip(jnp.cumsum(lb_sm, axis=0) - lb_sm[0], 0.0, 1.0 - 1e-6)
    xp, xs = x_prompt, x_sample
    kp, vp, sp, ksm, vsm, ssm = [], [], [], [], [], []
    for layer in range(DEPTH):
        j = layer // N_MIXERS
        hp = rms_norm(xp, norm_w[layer])
        hs = rms_norm(xs, norm_w[layer])
        if layer % N_MIXERS == 0:
            op, k1, v1 = attn_mixer(hp, w_in[layer], q_norm_w[j], k_norm_w[j], None, None)
            osm, k2, v2 = attn_mixer(hs, w_in[layer], q_norm_w[j], k_norm_w[j], cache_k[j], cache_v[j])
            kp.append(k1); vp.append(v1); ksm.append(k2); vsm.append(v2)
        else:
            s0 = jnp.zeros((xp.shape[0], N_HEADS, HEAD_DIM, HEAD_DIM), jnp.float32)
            op, s1 = hgrn_mixer(hp, w_in[layer], lbs[j], hgrn_norm_w[j], s0)
            osm, s2 = hgrn_mixer(hs, w_in[layer], lbs[j], hgrn_norm_w[j], state_hgrn[j])
            sp.append(s1); ssm.append(s2)
        xp = xp + op @ w_out[layer]
        xs = xs + osm @ w_out[layer]
    return (xp, xs, jnp.stack(kp), jnp.stack(vp), jnp.stack(sp), jnp.stack(ksm), jnp.stack(vsm), jnp.stack(ssm))
```

```python
import functools

import jax
import jax.numpy as jnp
from jax import lax
from jax.experimental import pallas as pl
from jax.experimental.pallas import tpu as pltpu

D_MODEL = 2048
N_HEADS = 16
HEAD_DIM = 128
N_PROJ = 4
DEPTH = 4
HGRN_CHUNK = 64
EPS = 1e-6
LB_FLOOR = 1e-30

F32 = jnp.float32
BF16 = jnp.bfloat16

VMEM_LIMIT_BYTES = 56 * 1024 * 1024

_NT = (((1,), (1,)), ((), ()))


def _params(*sem):
    return pltpu.CompilerParams(dimension_semantics=sem, vmem_limit_bytes=VMEM_LIMIT_BYTES)


def _softplus(z):
    return jnp.maximum(z, 0.0) + jnp.log(1.0 + jnp.exp(-jnp.abs(z)))


def _sigmoid(z):
    return 1.0 / (1.0 + jnp.exp(-z))


def _split_bf16(x):
    hi = x.astype(BF16)
    lo = (x - hi.astype(F32)).astype(BF16)
    return hi, lo


def _inproj_kernel(x_ref, nw_ref, w_ref, qk_ref, o_ref, h_ref, *, tm, tn, qk_norm, norm_rows):
    j = pl.program_id(1)

    @pl.when(j == 0)
    def _():
        for r in range(0, tm, norm_rows):
            x = x_ref[r:r + norm_rows, :]
            ms = jnp.mean(x * x, axis=-1, keepdims=True)
            h_ref[r:r + norm_rows, :] = (x * lax.rsqrt(ms + EPS) * nw_ref[...]).astype(BF16)

    acc = jnp.dot(h_ref[...], w_ref[...], preferred_element_type=F32)

    if not qk_norm:
        o_ref[...] = acc
        return

    blocks_per_group = D_MODEL // tn
    group = j // blocks_per_group

    @pl.when(group >= 2)
    def _():
        o_ref[...] = acc

    @pl.when(group < 2)
    def _():
        w = qk_ref[pl.ds(group, 1), :]
        for c in range(tn // HEAD_DIM):
            a = acc[:, c * HEAD_DIM:(c + 1) * HEAD_DIM]
            ms = jnp.mean(a * a, axis=-1, keepdims=True)
            o_ref[:, c * HEAD_DIM:(c + 1) * HEAD_DIM] = a * lax.rsqrt(ms + EPS) * w


def _in_proj(x2d, norm_w, w_bf16, qk_w):
    n = x2d.shape[0]
    tm = min(n, 1024)
    tn = 512
    qk_norm = qk_w is not None
    if qk_w is None:
        qk_w = jnp.ones((2, HEAD_DIM), F32)
    kern = functools.partial(_inproj_kernel, tm=tm, tn=tn, qk_norm=qk_norm, norm_rows=256)
    return pl.pallas_call(
        kern,
        grid=(n // tm, N_PROJ * D_MODEL // tn),
        in_specs=[
            pl.BlockSpec((tm, D_MODEL), lambda i, j: (i, 0)),
            pl.BlockSpec((1, D_MODEL), lambda i, j: (0, 0)),
            pl.BlockSpec((D_MODEL, tn), lambda i, j: (0, j)),
            pl.BlockSpec((2, HEAD_DIM), lambda i, j: (0, 0)),
        ],
        out_specs=pl.BlockSpec((tm, tn), lambda i, j: (i, j)),
        out_shape=jax.ShapeDtypeStruct((n, N_PROJ * D_MODEL), F32),
        scratch_shapes=[pltpu.VMEM((tm, D_MODEL), BF16)],
        compiler_params=_params("parallel", "arbitrary"),
        name="in_proj_qk" if qk_norm else "in_proj",
    )(x2d, norm_w.reshape(1, D_MODEL), w_bf16, qk_w)


def _outproj_kernel(x_ref, o_ref, w_ref, y_ref):
    y_ref[...] = x_ref[...] + jnp.dot(o_ref[...], w_ref[...], preferred_element_type=F32)


def _out_proj(x2d, o2d, w_bf16):
    n = x2d.shape[0]
    tm = min(n, 1024)
    tn = 512
    return pl.pallas_call(
        _outproj_kernel,
        grid=(n // tm, D_MODEL // tn),
        in_specs=[
            pl.BlockSpec((tm, tn), lambda i, j: (i, j)),
            pl.BlockSpec((tm, D_MODEL), lambda i, j: (i, 0)),
            pl.BlockSpec((D_MODEL, tn), lambda i, j: (0, j)),
        ],
        out_specs=pl.BlockSpec((tm, tn), lambda i, j: (i, j)),
        out_shape=jax.ShapeDtypeStruct((n, D_MODEL), F32),
        compiler_params=_params("parallel", "arbitrary"),
        name="out_proj",
    )(x2d, o2d, w_bf16)


def _sb_tile(q, kt, vt, tt, acc_ref, car_ref, masked):
    z = lax.dot_general(q, kt, _NT, preferred_element_type=F32)
    sp = _softplus(z)
    if masked:
        row = lax.broadcasted_iota(jnp.int32, z.shape, 0)
        col = lax.broadcasted_iota(jnp.int32, z.shape, 1)
        earlier = col < row
        sp = jnp.where(earlier, sp, 0.0)
    hi, lo = _split_bf16(sp)
    cs = (jnp.dot(hi, tt, preferred_element_type=F32)
          + jnp.dot(lo, tt, preferred_element_type=F32))
    w = jnp.exp(z - cs)
    if masked:
        w = jnp.where(earlier, w, 0.0)
    pv = jnp.dot(w.astype(BF16), vt, preferred_element_type=F32)
    car = car_ref[...]
    acc_ref[...] += jnp.exp(-car) * pv
    car_ref[...] = car + cs[:, 0:1]


def _attn_prompt_kernel(q_ref, k_ref, v_ref, g_ref, tt_ref, o_ref, kb_ref, vb_ref, acc_ref, car_ref, *, tq):
    qi = pl.program_id(2)

    @pl.when(qi == 0)
    def _():
        kb_ref[...] = k_ref[0].astype(BF16)
        vb_ref[...] = v_ref[0].astype(BF16)

    q = q_ref[0].astype(BF16)
    tt = tt_ref[...]
    acc_ref[...] = jnp.zeros_like(acc_ref)
    car_ref[...] = jnp.zeros_like(car_ref)

    def tile(kstart, masked):
        kstart = pl.multiple_of(kstart, tq)
        _sb_tile(q, kb_ref[pl.ds(kstart, tq), :], vb_ref[pl.ds(kstart, tq), :], tt, acc_ref, car_ref, masked)

    tile(qi * tq, True)

    def body(j, c):
        tile((qi - 1 - j) * tq, False)
        return c

    lax.fori_loop(0, qi, body, 0)

    g = g_ref[0]
    o_ref[0] = (acc_ref[...] * (g * _sigmoid(g))).astype(BF16)


def _tri_suffix(n):
    r = jnp.arange(n)
    return (r[:, None] >= r[None, :]).astype(BF16)


def _attn_prompt(proj, tq=256):
    b, t, _ = proj.shape
    h = N_HEADS
    kern = functools.partial(_attn_prompt_kernel, tq=tq)
    return pl.pallas_call(
        kern,
        grid=(b, h, t // tq),
        in_specs=[
            pl.BlockSpec((1, tq, HEAD_DIM), lambda bi, hi, qi: (bi, qi, hi)),
            pl.BlockSpec((1, t, HEAD_DIM), lambda bi, hi, qi: (bi, 0, h + hi)),
            pl.BlockSpec((1, t, HEAD_DIM), lambda bi, hi, qi: (bi, 0, 2 * h + hi)),
            pl.BlockSpec((1, tq, HEAD_DIM), lambda bi, hi, qi: (bi, qi, 3 * h + hi)),
            pl.BlockSpec((tq, tq), lambda bi, hi, qi: (0, 0)),
        ],
        out_specs=pl.BlockSpec((1, tq, HEAD_DIM), lambda bi, hi, qi: (bi, qi, hi)),
        out_shape=jax.ShapeDtypeStruct((b, t, D_MODEL), BF16),
        scratch_shapes=[
            pltpu.VMEM((t, HEAD_DIM), BF16),
            pltpu.VMEM((t, HEAD_DIM), BF16),
            pltpu.VMEM((tq, HEAD_DIM), F32),
            pltpu.VMEM((tq, 1), F32),
        ],
        compiler_params=_params("parallel", "parallel", "arbitrary"),
        name="attn_prompt",
    )(proj, proj, proj, proj, _tri_suffix(tq))


def _attn_sample_kernel(q_ref, k_ref, v_ref, g_ref, pk_ref, pv_ref, ttd_ref, ttp_ref, o_ref, acc_ref, car_ref,
                        *, tk, n_past):
    q = q_ref[0].astype(BF16)
    acc_ref[...] = jnp.zeros_like(acc_ref)
    car_ref[...] = jnp.zeros_like(car_ref)

    _sb_tile(q, k_ref[0].astype(BF16), v_ref[0].astype(BF16), ttd_ref[...], acc_ref, car_ref, True)

    ttp = ttp_ref[...]

    def body(j, c):
        kstart = pl.multiple_of((n_past - 1 - j) * tk, tk)
        kt = pk_ref[0, pl.ds(kstart, tk), :].astype(BF16)
        vt = pv_ref[0, pl.ds(kstart, tk), :].astype(BF16)
        _sb_tile(q, kt, vt, ttp, acc_ref, car_ref, False)
        return c

    lax.fori_loop(0, n_past, body, 0)

    g = g_ref[0]
    o_ref[0] = (acc_ref[...] * (g * _sigmoid(g))).astype(BF16)


def _attn_sample(proj, past_k, past_v, tk=256):
    b, tq, _ = proj.shape
    p = past_k.shape[1]
    h = N_HEADS
    kern = functools.partial(_attn_sample_kernel, tk=tk, n_past=p // tk)
    cur = lambda off: pl.BlockSpec((1, tq, HEAD_DIM), lambda bi, hi: (bi, 0, off * h + hi))
    past = pl.BlockSpec((1, p, HEAD_DIM), lambda bi, hi: (bi, 0, hi))
    return pl.pallas_call(
        kern,
        grid=(b, h),
        in_specs=[cur(0), cur(1), cur(2), cur(3), past, past,
                  pl.BlockSpec((tq, tq), lambda bi, hi: (0, 0)),
                  pl.BlockSpec((tk, tk), lambda bi, hi: (0, 0))],
        out_specs=pl.BlockSpec((1, tq, HEAD_DIM), lambda bi, hi: (bi, 0, hi)),
        out_shape=jax.ShapeDtypeStruct((b, tq, D_MODEL), BF16),
        scratch_shapes=[pltpu.VMEM((tq, HEAD_DIM), F32), pltpu.VMEM((tq, 1), F32)],
        compiler_params=_params("parallel", "parallel"),
        name="attn_sample",
    )(proj, proj, proj, proj, past_k, past_v, _tri_suffix(tq), _tri_suffix(tk))


def _hgrn_kernel(qa_ref, fa_ref, ia_ref, g_ref, lb_ref, nw_ref, cum_ref, s0_ref, o_ref, s_ref,
                 st_ref, eg_pad, k_pad, v_pad, *, rows, chunk, layer_j, pad):
    t = pl.program_id(2)
    nt = pl.num_programs(2)

    @pl.when(t == 0)
    def _():
        st_ref[...] = s0_ref[0, 0].T

    lraw = lb_ref[...]
    e = jnp.exp(lraw - jnp.max(lraw, axis=0, keepdims=True))
    sm = e / jnp.sum(e, axis=0, keepdims=True)
    lb = jnp.sum(sm[0:layer_j + 1], axis=0, keepdims=True) - sm[0:1]
    lb = jnp.clip(lb, 0.0, 1.0 - 1e-6)

    qa = qa_ref[0]
    fa = fa_ref[0]
    v = ia_ref[0]
    q = qa * _sigmoid(qa)
    ls = -_softplus(-fa)
    a = jnp.log(jnp.maximum(lb, LB_FLOOR))
    c = jnp.log(1.0 - lb) + ls
    logf = jnp.maximum(a, c) + jnp.log(1.0 + jnp.exp(-jnp.abs(a - c)))
    logf = jnp.minimum(logf, 0.0)
    kk = (1.0 - lb) * _sigmoid(-fa)

    ghi, glo = _split_bf16(logf)
    cum = cum_ref[...]
    bcum = jnp.dot(cum, ghi, preferred_element_type=F32) + jnp.dot(cum, glo, preferred_element_type=F32)

    o_parts = []
    for ci in range(rows // chunk):
        sl = slice(ci * chunk, (ci + 1) * chunk)
        bc = bcum[sl]
        bl = bc[chunk - 1:chunk]
        st = st_ref[...]
        qt = (q[sl] * jnp.exp(bc)).astype(BF16)
        o_parts.append(lax.dot_general(qt, st.astype(BF16), _NT, preferred_element_type=F32))
        kt = (kk[sl] * jnp.exp(bl - bc)).astype(BF16)
        ut = jnp.dot(v[sl].T.astype(BF16), kt, preferred_element_type=F32)
        st_ref[...] = st * jnp.exp(bl) + ut
    o = jnp.concatenate(o_parts, axis=0) if len(o_parts) > 1 else o_parts[0]

    rid = lax.broadcasted_iota(jnp.int32, (rows, HEAD_DIM), 0)
    egz = jnp.where(rid % chunk == 0, 0.0, jnp.exp(logf))
    zpad = jnp.zeros((pad, HEAD_DIM), F32)
    eg_pad[0:pad, :] = zpad
    k_pad[0:pad, :] = zpad
    v_pad[0:pad, :] = zpad
    eg_pad[pad:pad + rows, :] = egz
    k_pad[pad:pad + rows, :] = kk
    v_pad[pad:pad + rows, :] = v
    ones = jnp.ones((HEAD_DIM, HEAD_DIM), BF16)
    decay = None
    for d in range(chunk):
        if d == 0:
            p = q * kk
            vd = v
        else:
            egd = eg_pad[pad - (d - 1):pad - (d - 1) + rows, :]
            decay = egd if decay is None else decay * egd
            p = q * k_pad[pad - d:pad - d + rows, :] * decay
            vd = v_pad[pad - d:pad - d + rows, :]
        r = jnp.dot(p.astype(BF16), ones, preferred_element_type=F32)
        o = o + r * vd

    ms = jnp.mean(o * o, axis=-1, keepdims=True)
    y = o * lax.rsqrt(ms + EPS) * nw_ref[...]
    g = g_ref[0]
    o_ref[0] = (y * (g * _sigmoid(g))).astype(BF16)

    @pl.when(t == nt - 1)
    def _():
        s_ref[0, 0] = st_ref[...].T


def _hgrn(proj, lb_raw, layer_j, out_w, s0, rows):
    b, t, _ = proj.shape
    h = N_HEADS
    chunk = min(HGRN_CHUNK, rows)
    pad = max(8, ((chunk - 1 + 7) // 8) * 8)
    r = jnp.arange(rows)
    cum = ((r[:, None] >= r[None, :]) & (r[:, None] // chunk == r[None, :] // chunk)).astype(BF16)
    kern = functools.partial(_hgrn_kernel, rows=rows, chunk=chunk, layer_j=layer_j, pad=pad)
    blk = lambda off: pl.BlockSpec((1, rows, HEAD_DIM), lambda bi, hi, ti: (bi, ti, off * h + hi))
    n_l = lb_raw.shape[0]
    return pl.pallas_call(
        kern,
        grid=(b, h, t // rows),
        in_specs=[blk(0), blk(1), blk(2), blk(3),
                  pl.BlockSpec((n_l, HEAD_DIM), lambda bi, hi, ti: (0, hi)),
                  pl.BlockSpec((1, HEAD_DIM), lambda bi, hi, ti: (0, 0)),
                  pl.BlockSpec((rows, rows), lambda bi, hi, ti: (0, 0)),
                  pl.BlockSpec((1, 1, HEAD_DIM, HEAD_DIM), lambda bi, hi, ti: (bi, hi, 0, 0))],
        out_specs=[pl.BlockSpec((1, rows, HEAD_DIM), lambda bi, hi, ti: (bi, ti, hi)),
                   pl.BlockSpec((1, 1, HEAD_DIM, HEAD_DIM), lambda bi, hi, ti: (bi, hi, 0, 0))],
        out_shape=[jax.ShapeDtypeStruct((b, t, D_MODEL), BF16),
                   jax.ShapeDtypeStruct((b, h, HEAD_DIM, HEAD_DIM), F32)],
        scratch_shapes=[pltpu.VMEM((HEAD_DIM, HEAD_DIM), F32),
                        pltpu.VMEM((pad + rows, HEAD_DIM), F32),
                        pltpu.VMEM((pad + rows, HEAD_DIM), F32),
                        pltpu.VMEM((pad + rows, HEAD_DIM), F32)],
        compiler_params=_params("parallel", "parallel", "arbitrary"),
        name="hgrn",
    )(proj, proj, proj, proj, lb_raw, out_w.reshape(1, HEAD_DIM), cum, s0)


def kernel(x_prompt, x_sample, cache_k, cache_v, state_hgrn, norm_w, w_in, w_out,
           q_norm_w, k_norm_w, hgrn_norm_w, hgrn_lb):
    bp, tp, d = x_prompt.shape
    bs, ts, _ = x_sample.shape
    h, dh = N_HEADS, HEAD_DIM
    xp = x_prompt.reshape(bp * tp, d)
    xs = x_sample.reshape(bs * ts, d)
    w_in_b = w_in.astype(BF16)
    w_out_b = w_out.astype(BF16)
    lb_raw = hgrn_lb.astype(F32)

    kp, vp, sp, ksm, vsm, ssm = [], [], [], [], [], []
    for layer in range(DEPTH):
        j = layer // 2
        if layer % 2 == 0:
            qk_w = jnp.stack([q_norm_w[j] * (dh ** -0.5), k_norm_w[j]]).astype(F32)
            pp = _in_proj(xp, norm_w[layer], w_in_b[layer], qk_w).reshape(bp, tp, N_PROJ * d)
            ps = _in_proj(xs, norm_w[layer], w_in_b[layer], qk_w).reshape(bs, ts, N_PROJ * d)
            kp.append(pp[:, :, d:2 * d].reshape(bp, tp, h, dh))
            vp.append(pp[:, :, 2 * d:3 * d].reshape(bp, tp, h, dh))
            ksm.append(ps[:, :, d:2 * d].reshape(bs, ts, h, dh))
            vsm.append(ps[:, :, 2 * d:3 * d].reshape(bs, ts, h, dh))
            op = _attn_prompt(pp)
            past_len = cache_k.shape[2]
            osm = _attn_sample(ps, cache_k[j].reshape(bs, past_len, d), cache_v[j].reshape(bs, past_len, d))
        else:
            pp = _in_proj(xp, norm_w[layer], w_in_b[layer], None).reshape(bp, tp, N_PROJ * d)
            ps = _in_proj(xs, norm_w[layer], w_in_b[layer], None).reshape(bs, ts, N_PROJ * d)
            s0 = jnp.zeros((bp, h, dh, dh), F32)
            op, s1 = _hgrn(pp, lb_raw, j, hgrn_norm_w[j], s0, rows=256)
            osm, s2 = _hgrn(ps, lb_raw, j, hgrn_norm_w[j], state_hgrn[j].astype(F32), rows=ts)
            sp.append(s1)
            ssm.append(s2)
        xp = _out_proj(xp, op.reshape(bp * tp, d), w_out_b[layer])
        xs = _out_proj(xs, osm.reshape(bs * ts, d), w_out_b[layer])

    return (xp.reshape(bp, tp, d), xs.reshape(bs, ts, d), jnp.stack(kp), jnp.stack(vp), jnp.stack(sp),
            jnp.stack(ksm), jnp.stack(vsm), jnp.stack(ssm))
```
